```python
import math
import jax, jax.numpy as jnp
from jax import lax
import numpy as np

D_MODEL = 1024
BATCH = 2
SEQ = 8192
DEPTH = 1

N_MEM = 256
MLA_HEADS = 8
QK_NOPE = 64
QK_ROPE = 32
V_HEAD = 64
Q_LORA = 384
KV_LORA = 256
MLA_WIDTH = MLA_HEADS * V_HEAD
SSM_WIDTH = 512
SSM_GROUP = 16
SSM_GROUPS = SSM_WIDTH // SSM_GROUP
SSM_STATE = 64
DT_MIN = 1e-3
DT_MAX = 1e-1
MIX_WIDTH = MLA_WIDTH + SSM_WIDTH
IN_WIDTH = Q_LORA + KV_LORA + QK_ROPE + SSM_WIDTH
X_HEADS = 4
X_HEAD_DIM = D_MODEL // X_HEADS
N_KEYS = 128
N_EXPERTS = N_KEYS * N_KEYS
PEER_HEADS = 8
PEER_TOPK = 16
D_KEY = 256
HALF_KEY = D_KEY // 2
Q_BLOCK = 128
TOK_BLOCK = 128
ROPE_THETA = 10000.0
LN_EPS = 1e-5
RMS_EPS = 1e-6
ALPHA = (2 * DEPTH) ** 0.25
BETA = (8 * DEPTH) ** -0.25

kernel_name = "hybrid_mla_s5_peer_deepnorm_layer"


def layer_norm(x, g, b):
    xf = x.astype(jnp.float32)
    mu = jnp.mean(xf, -1, keepdims=True)
    var = jnp.mean(jnp.square(xf - mu), -1, keepdims=True)
    return ((xf - mu) * lax.rsqrt(var + LN_EPS) * g + b).astype(x.dtype)


def rms_norm(x, g):
    xf = x.astype(jnp.float32)
    return (xf * lax.rsqrt(jnp.mean(jnp.square(xf), -1, keepdims=True) + RMS_EPS) * g).astype(x.dtype)


def rope_angles(pos, dim):
    half = dim // 2
    freqs = ROPE_THETA ** (-jnp.arange(half, dtype=jnp.float32) / half)
    ang = pos.astype(jnp.float32)[..., None] * freqs
    return jnp.cos(ang), jnp.sin(ang)


def apply_rope(x, cos, sin):
    half = x.shape[-1] // 2
    xf = x.astype(jnp.float32)
    x1, x2 = xf[..., :half], xf[..., half:]
    return jnp.concatenate([x1 * cos - x2 * sin, x2 * cos + x1 * sin], -1).astype(x.dtype)


def mla(c_q, c_kv, k_r, pos, g_cq, w_uq, g_ckv, w_ukv):
    B, L, _ = c_q.shape
    dq = QK_NOPE + QK_ROPE
    q = (rms_norm(c_q, g_cq) @ w_uq).reshape(B, L, MLA_HEADS, dq)
    kv = (rms_norm(c_kv, g_ckv) @ w_ukv).reshape(B, L, MLA_HEADS, QK_NOPE + V_HEAD)
    q_nope, q_rope = q[..., :QK_NOPE], q[..., QK_NOPE:]
    k_nope, v = kv[..., :QK_NOPE], kv[..., QK_NOPE:]
    cos, sin = rope_angles(pos, QK_ROPE)
    q_rope = apply_rope(q_rope, cos[:, :, None], sin[:, :, None])
    k_rope = apply_rope(k_r, cos, sin)
    q = jnp.concatenate([q_nope, q_rope], -1)
    k = jnp.concatenate([k_nope, jnp.broadcast_to(k_rope[:, :, None], (B, L, MLA_HEADS, QK_ROPE))], -1)
    scale = dq ** -0.5
    nb = L // Q_BLOCK
    q_blocks = q.reshape(B, nb, Q_BLOCK, MLA_HEADS, dq).transpose(1, 0, 2, 3, 4)
    key_idx = jnp.arange(L)

    def block(args):
        qb, start = args
        s = jnp.einsum('bqhd,bkhd->bhqk', qb, k).astype(jnp.float32) * scale
        q_idx = start + jnp.arange(Q_BLOCK)
        s = jnp.where(key_idx[None, :] <= q_idx[:, None], s, -jnp.inf)
        p = jax.nn.softmax(s, -1).astype(v.dtype)
        return jnp.einsum('bhqk,bkhd->bqhd', p, v)

    o = lax.map(block, (q_blocks, jnp.arange(nb) * Q_BLOCK))
    return o.transpose(1, 0, 2, 3, 4).reshape(B, L, MLA_WIDTH)


def s5_ssm(u, lam_re, lam_im, log_dt, b_re, b_im, c_re, c_im, d_skip, w_glu, b_glu):
    B, L, _ = u.shape
    f32 = jnp.float32
    ug = u.reshape(B, L, SSM_GROUPS, SSM_GROUP).astype(f32)
    lr, li = lam_re.astype(f32), lam_im.astype(f32)
    b_re, b_im = b_re.astype(f32), b_im.astype(f32)
    c_re, c_im = c_re.astype(f32), c_im.astype(f32)
    dt = jnp.exp(log_dt.astype(f32))[:, None]
    mag = jnp.exp(lr * dt)
    ab_re, ab_im = mag * jnp.cos(li * dt), mag * jnp.sin(li * dt)
    den = lr * lr + li * li
    nr, ni = ab_re - 1.0, ab_im
    f_re = (nr * lr + ni * li) / den
    f_im = (ni * lr - nr * li) / den
    bb_re = f_re[..., None] * b_re - f_im[..., None] * b_im
    bb_im = f_re[..., None] * b_im + f_im[..., None] * b_re
    bu_re = jnp.einsum('blgh,gph->blgp', ug, bb_re)
    bu_im = jnp.einsum('blgh,gph->blgp', ug, bb_im)
    a_re = jnp.broadcast_to(ab_re, bu_re.shape)
    a_im = jnp.broadcast_to(ab_im, bu_im.shape)

    def combine(e1, e2):
        a1r, a1i, b1r, b1i = e1
        a2r, a2i, b2r, b2i = e2
        return (a2r * a1r - a2i * a1i,
                a2r * a1i + a2i * a1r,
                a2r * b1r - a2i * b1i + b2r,
                a2r * b1i + a2i * b1r + b2i)

    _, _, s_re, s_im = lax.associative_scan(combine, (a_re, a_im, bu_re, bu_im), axis=1)
    y = (jnp.einsum('blgp,ghp->blgh', s_re, c_re)
         - jnp.einsum('blgp,ghp->blgh', s_im, c_im)
         + d_skip.astype(f32) * ug)
    y = jax.nn.gelu(y.reshape(B, L, SSM_WIDTH))
    y = y * jax.nn.sigmoid(y @ w_glu.astype(f32) + b_glu.astype(f32))
    return y.astype(u.dtype)


def mem_xattn(x, mem, w_q, w_k, w_v, w_o):
    B, L, _ = x.shape
    M = mem.shape[1]
    q = (x @ w_q).reshape(B, L, X_HEADS, X_HEAD_DIM)
    k = (mem @ w_k).reshape(B, M, X_HEADS, X_HEAD_DIM)
    v = (mem @ w_v).reshape(B, M, X_HEADS, X_HEAD_DIM)
    s = jnp.einsum('blhd,bmhd->bhlm', q, k).astype(jnp.float32) * (X_HEAD_DIM ** -0.5)
    p = jax.nn.softmax(s, -1).astype(v.dtype)
    o = jnp.einsum('bhlm,bmhd->blhd', p, v).reshape(B, L, D_MODEL)
    return o @ w_o


def peer(x, w_query, sub_keys1, sub_keys2, u_tab, v_tab):
    B, L, D = x.shape
    T = B * L
    xt = x.reshape(T, D)
    q = (xt @ w_query).reshape(T, PEER_HEADS, 2, HALF_KEY)
    s1 = jnp.einsum('thd,nd->thn', q[:, :, 0], sub_keys1).astype(jnp.float32)
    s2 = jnp.einsum('thd,nd->thn', q[:, :, 1], sub_keys2).astype(jnp.float32)
    v1, i1 = lax.top_k(s1, PEER_TOPK)
    v2, i2 = lax.top_k(s2, PEER_TOPK)
    cand_s = (v1[..., :, None] + v2[..., None, :]).reshape(T, PEER_HEADS, PEER_TOPK * PEER_TOPK)
    cand_id = (i1[..., :, None] * N_KEYS + i2[..., None, :]).reshape(T, PEER_HEADS, PEER_TOPK * PEER_TOPK)
    top_s, sel = lax.top_k(cand_s, PEER_TOPK)
    ids = jnp.take_along_axis(cand_id, sel, -1)
    gates = jax.nn.softmax(top_s, -1).astype(x.dtype)
    nb = T // TOK_BLOCK

    def block(args):
        xb, idb, gb = args
        u = u_tab[idb]
        h = jax.nn.gelu(jnp.einsum('td,thkd->thk', xb, u))
        return jnp.einsum('thk,thkd->td', gb * h, v_tab[idb])

    out = lax.map(block, (xt.reshape(nb, TOK_BLOCK, D),
                          ids.reshape(nb, TOK_BLOCK, PEER_HEADS, PEER_TOPK),
                          gates.reshape(nb, TOK_BLOCK, PEER_HEADS, PEER_TOPK)))
    return out.reshape(B, L, D)


def setup_inputs(seed: int = 0) -> dict:
    key = jax.random.key(seed)
    ks = iter(jax.random.split(key, 48))

    def nrm(shape, scale):
        return jax.random.normal(next(ks), shape, jnp.float32) * scale

    def gain(n):
        return 1.0 + nrm((DEPTH, n), 0.02)

    G, P, H = SSM_GROUPS, SSM_STATE, SSM_GROUP
    x = nrm((BATCH, SEQ, D_MODEL), 1.0)
    mem = nrm((BATCH, N_MEM, D_MODEL), 1.0)
    positions = (jax.random.randint(next(ks), (BATCH, 1), 0, 4096, dtype=jnp.int32)
                 + jnp.arange(SEQ, dtype=jnp.int32)[None, :])
    n_idx = jnp.arange(P, dtype=jnp.float32)
    return {
        "x": x,
        "mem": mem,
        "positions": positions,
        "w_in": nrm((DEPTH, D_MODEL, IN_WIDTH), D_MODEL ** -0.5),
        "g_cq": gain(Q_LORA),
        "w_uq": nrm((DEPTH, Q_LORA, MLA_HEADS * (QK_NOPE + QK_ROPE)), Q_LORA ** -0.5),
        "g_ckv": gain(KV_LORA),
        "w_ukv": nrm((DEPTH, KV_LORA, MLA_HEADS * (QK_NOPE + V_HEAD)), KV_LORA ** -0.5),
        "lam_re": -0.5 + nrm((DEPTH, G, P), 0.01),
        "lam_im": jnp.pi * n_idx + nrm((DEPTH, G, P), 0.01),
        "log_dt": jax.random.uniform(next(ks), (DEPTH, G), jnp.float32,
                                     minval=math.log(DT_MIN), maxval=math.log(DT_MAX)),
        "b_re": nrm((DEPTH, G, P, H), (2.0 * H) ** -0.5),
        "b_im": nrm((DEPTH, G, P, H), (2.0 * H) ** -0.5),
        "c_re": nrm((DEPTH, G, H, P), (2.0 * P) ** -0.5),
        "c_im": nrm((DEPTH, G, H, P), (2.0 * P) ** -0.5),
        "d_skip": nrm((DEPTH, G, H), 1.0),
        "w_glu": nrm((DEPTH, SSM_WIDTH, SSM_WIDTH), SSM_WIDTH ** -0.5),
        "b_glu": nrm((DEPTH, SSM_WIDTH), 0.01),
        "g_mla_out": gain(MLA_WIDTH),
        "g_ssm_out": gain(SSM_WIDTH),
        "w_out": nrm((DEPTH, MIX_WIDTH, D_MODEL), BETA * MIX_WIDTH ** -0.5),
        "ln1_g": gain(D_MODEL),
        "ln1_b": nrm((DEPTH, D_MODEL), 0.01),
        "xq": nrm((DEPTH, D_MODEL, D_MODEL), D_MODEL ** -0.5),
        "xk": nrm((DEPTH, D_MODEL, D_MODEL), D_MODEL ** -0.5),
        "xv": nrm((DEPTH, D_MODEL, D_MODEL), D_MODEL ** -0.5),
        "xo": nrm((DEPTH, D_MODEL, D_MODEL), BETA * D_MODEL ** -0.5),
        "ln2_g": gain(D_MODEL),
        "ln2_b": nrm((DEPTH, D_MODEL), 0.01),
        "w_query": nrm((DEPTH, D_MODEL, PEER_HEADS * D_KEY), D_MODEL ** -0.5),
        "sub_keys1": nrm((DEPTH, N_KEYS, HALF_KEY), HALF_KEY ** -0.5),
        "sub_keys2": nrm((DEPTH, N_KEYS, HALF_KEY), HALF_KEY ** -0.5),
        "u_tab": nrm((DEPTH, N_EXPERTS, D_MODEL), D_MODEL ** -0.5),
        "v_tab": nrm((DEPTH, N_EXPERTS, D_MODEL), BETA),
        "ln3_g": gain(D_MODEL),
        "ln3_b": nrm((DEPTH, D_MODEL), 0.01),
    }


def reference(x, mem, positions, w_in, g_cq, w_uq, g_ckv, w_ukv, lam_re, lam_im, log_dt,
              b_re, b_im, c_re, c_im, d_skip, w_glu, b_glu, g_mla_out, g_ssm_out, w_out,
              ln1_g, ln1_b, xq, xk, xv, xo, ln2_g, ln2_b, w_query, sub_keys1, sub_keys2,
              u_tab, v_tab, ln3_g, ln3_b):
    split_at = [Q_LORA, Q_LORA + KV_LORA, Q_LORA + KV_LORA + QK_ROPE]
    for l in range(DEPTH):
        proj = x @ w_in[l]
        c_q, c_kv, k_r, u = jnp.split(proj, split_at, axis=-1)
        o_attn = mla(c_q, c_kv, k_r, positions, g_cq[l], w_uq[l], g_ckv[l], w_ukv[l])
        o_ssm = s5_ssm(u, lam_re[l], lam_im[l], log_dt[l], b_re[l], b_im[l], c_re[l], c_im[l],
                       d_skip[l], w_glu[l], b_glu[l])
        mixed = jnp.concatenate([rms_norm(o_attn, g_mla_out[l]),
                                 rms_norm(o_ssm, g_ssm_out[l])], -1) @ w_out[l]
        x = layer_norm(ALPHA * x + mixed, ln1_g[l], ln1_b[l])
        x = layer_norm(ALPHA * x + mem_xattn(x, mem, xq[l], xk[l], xv[l], xo[l]), ln2_g[l], ln2_b[l])
        x = layer_norm(ALPHA * x + peer(x, w_query[l], sub_keys1[l], sub_keys2[l], u_tab[l], v_tab[l]),
                       ln3_g[l], ln3_b[l])
    return x
```

```python
import functools
import math

import jax
import jax.numpy as jnp
from jax import lax
from jax.experimental import pallas as pl
from jax.experimental.pallas import tpu as pltpu

F32 = jnp.float32
BF16 = jnp.bfloat16

D_MODEL = 1024
MLA_HEADS = 8
QK_NOPE = 64
QK_ROPE = 32
V_HEAD = 64
Q_LORA = 384
KV_LORA = 256
MLA_WIDTH = MLA_HEADS * V_HEAD
SSM_WIDTH = 512
SSM_GROUP = 16
SSM_GROUPS = SSM_WIDTH // SSM_GROUP
SSM_STATE = 64
X_HEADS = 4
X_HEAD_DIM = D_MODEL // X_HEADS
N_KEYS = 128
N_EXPERTS = N_KEYS * N_KEYS
PEER_HEADS = 8
PEER_TOPK = 16
D_KEY = 256
HALF_KEY = D_KEY // 2
ROPE_THETA = 10000.0
LN_EPS = 1e-5
RMS_EPS = 1e-6
DEPTH = 1
ALPHA = (2 * DEPTH) ** 0.25

LANES = 128
HEAD_PAD = 128
SSM_CHUNK = 16
CHUNK_W = SSM_CHUNK * SSM_GROUP
VMEM_LIMIT = 56 * 1024 * 1024

NEG_INF = float("-inf")


def _cparams(*sem):
    return pltpu.CompilerParams(dimension_semantics=sem, vmem_limit_bytes=VMEM_LIMIT)


def _gelu_tanh(x):
    c = math.sqrt(2.0 / math.pi)
    return 0.5 * x * (1.0 + jnp.tanh(c * (x + 0.044715 * (x * x * x))))


def _rms(x, g, n):
    ms = jnp.sum(x * x, axis=-1, keepdims=True) * (1.0 / n)
    return x * lax.rsqrt(ms + RMS_EPS) * g


def _layer_norm_rows(z, g, b):
    mu = jnp.mean(z, axis=-1, keepdims=True)
    zc = z - mu
    var = jnp.mean(zc * zc, axis=-1, keepdims=True)
    return zc * lax.rsqrt(var + LN_EPS) * g + b


def _rope_kernel(pos_ref, cos_ref, sin_ref):
    half = QK_ROPE // 2
    lane = lax.broadcasted_iota(jnp.int32, pos_ref.shape, 1)
    k = (lane % half).astype(F32)
    freq = jnp.exp(k * (-math.log(ROPE_THETA) / half))
    ang = pos_ref[...].astype(F32) * freq
    cos_ref[...] = jnp.cos(ang)
    sin_ref[...] = jnp.sin(ang)


def _rope_tables(positions):
    half = QK_ROPE // 2
    t = positions.size
    per_row = LANES // half
    pos = jnp.repeat(positions.reshape(t // per_row, per_row), half, axis=1)
    cos, sin = pl.pallas_call(
        _rope_kernel,
        out_shape=(jax.ShapeDtypeStruct(pos.shape, F32),) * 2,
        name="rope_tables",
    )(pos)
    cos = cos.reshape(t, half)
    sin = sin.reshape(t, half)
    pad = jnp.zeros((t, HEAD_PAD - QK_NOPE - QK_ROPE), F32)
    cos_t = jnp.concatenate([jnp.ones((t, QK_NOPE), F32), cos, cos, pad], axis=1)
    sin_t = jnp.concatenate([jnp.zeros((t, QK_NOPE), F32), sin, sin, pad], axis=1)
    return cos_t, sin_t


def _in_proj_kernel(x_ref, cos_ref, sin_ref, w_in_ref, gq_ref, wq_ref, gkv_ref, wkv_ref,
                    q_ref, k_ref, v_ref, u_ref):
    hw = MLA_HEADS * HEAD_PAD
    proj = jnp.dot(x_ref[...].astype(BF16), w_in_ref[...], preferred_element_type=F32)
    cos = cos_ref[...]
    sin = sin_ref[...]
    o_kv = Q_LORA
    o_u = Q_LORA + KV_LORA
    o_kr = o_u + SSM_WIDTH
    cq = _rms(proj[:, :Q_LORA], gq_ref[...], Q_LORA).astype(BF16)
    ckv = _rms(proj[:, o_kv:o_u], gkv_ref[...], KV_LORA).astype(BF16)
    u_ref[...] = proj[:, o_u:o_kr]
    k_rope = proj[:, o_kr:o_kr + HEAD_PAD] * cos + proj[:, o_kr + HEAD_PAD:o_kr + 2 * HEAD_PAD] * sin
    qx = jnp.dot(cq, wq_ref[...], preferred_element_type=F32)
    kvx = jnp.dot(ckv, wkv_ref[...], preferred_element_type=F32)
    scale = (QK_NOPE + QK_ROPE) ** -0.5
    for h in range(MLA_HEADS):
        lo, hi = h * HEAD_PAD, (h + 1) * HEAD_PAD
        q = qx[:, lo:hi] * cos + qx[:, hw + lo:hw + hi] * sin
        q_ref[h] = (q * scale).astype(BF16)
        k_ref[h] = (kvx[:, lo:hi] + k_rope).astype(BF16)
        v_ref[h] = kvx[:, hw + lo:hw + hi].astype(BF16)


def _pad_heads(w, width, lo, hi):
    n = w.shape[0]
    w = w.reshape(n, MLA_HEADS, width)[:, :, lo:hi]
    w = jnp.pad(w, ((0, 0), (0, 0), (0, HEAD_PAD - (hi - lo))))
    return w.reshape(n, MLA_HEADS * HEAD_PAD)


def _rope_partner(w, off):
    half = QK_ROPE // 2
    x1 = w[..., off:off + half]
    x2 = w[..., off + half:off + 2 * half]
    return x1, x2, jnp.concatenate([-x2, x1], axis=-1)


def _in_proj(x2d, cos_t, sin_t, w_in, g_cq, w_uq, g_ckv, w_ukv, tile):
    t = x2d.shape[0]
    dq = QK_NOPE + QK_ROPE
    o_kr = Q_LORA + KV_LORA
    w_kr = w_in[:, o_kr:o_kr + QK_ROPE]
    _, _, kr_sw = _rope_partner(w_kr, 0)
    lane_pad = ((0, 0), (QK_NOPE, HEAD_PAD - dq))
    w_in_ext = jnp.concatenate(
        [w_in[:, :o_kr], w_in[:, o_kr + QK_ROPE:], jnp.pad(w_kr, lane_pad), jnp.pad(kr_sw, lane_pad)],
        axis=1).astype(BF16)
    wq3 = w_uq.reshape(Q_LORA, MLA_HEADS, dq)
    _, _, q_sw = _rope_partner(wq3, QK_NOPE)
    q_sw = jnp.pad(q_sw, ((0, 0), (0, 0), (QK_NOPE, HEAD_PAD - dq))).reshape(Q_LORA, MLA_HEADS * HEAD_PAD)
    wq = jnp.concatenate([_pad_heads(w_uq, dq, 0, dq), q_sw], axis=1).astype(BF16)
    wkv = jnp.concatenate([_pad_heads(w_ukv, QK_NOPE + V_HEAD, 0, QK_NOPE),
                           _pad_heads(w_ukv, QK_NOPE + V_HEAD, QK_NOPE, QK_NOPE + V_HEAD)],
                          axis=1).astype(BF16)
    n_ext = w_in_ext.shape[1]
    hw = MLA_HEADS * HEAD_PAD
    row = lambda i: (i, 0)
    full = lambda i: (0, 0)
    head_spec = pl.BlockSpec((MLA_HEADS, tile, HEAD_PAD), lambda i: (0, i, 0))
    head_shape = jax.ShapeDtypeStruct((MLA_HEADS, t, HEAD_PAD), BF16)
    return pl.pallas_call(
        _in_proj_kernel,
        grid=(t // tile,),
        in_specs=[
            pl.BlockSpec((tile, D_MODEL), row),
            pl.BlockSpec((tile, HEAD_PAD), row),
            pl.BlockSpec((tile, HEAD_PAD), row),
            pl.BlockSpec((D_MODEL, n_ext), full),
            pl.BlockSpec((1, Q_LORA), full),
            pl.BlockSpec((Q_LORA, 2 * hw), full),
            pl.BlockSpec((1, KV_LORA), full),
            pl.BlockSpec((KV_LORA, 2 * hw), full),
        ],
        out_specs=[head_spec, head_spec, head_spec, pl.BlockSpec((tile, SSM_WIDTH), row)],
        out_shape=[head_shape, head_shape, head_shape, jax.ShapeDtypeStruct((t, SSM_WIDTH), F32)],
        compiler_params=_cparams("parallel"),
        name="in_proj",
    )(x2d, cos_t, sin_t, w_in_ext, g_cq.reshape(1, -1), wq, g_ckv.reshape(1, -1), wkv)


def _flash_kernel(q_ref, k_ref, v_ref, o_ref, m_scr, l_scr, acc_scr, *, tile):
    qi = pl.program_id(2)
    ki = pl.program_id(3)

    @pl.when(ki == 0)
    def _():
        m_scr[...] = jnp.full(m_scr.shape, NEG_INF, F32)
        l_scr[...] = jnp.zeros(l_scr.shape, F32)
        acc_scr[...] = jnp.zeros(acc_scr.shape, F32)

    def update(masked):
        s = lax.dot_general(q_ref[0], k_ref[0], (((1,), (1,)), ((), ())), preferred_element_type=F32)
        if masked:
            row = lax.broadcasted_iota(jnp.int32, s.shape, 0)
            col = lax.broadcasted_iota(jnp.int32, s.shape, 1)
            s = jnp.where(col <= row, s, NEG_INF)
        m_prev = m_scr[...]
        m_new = jnp.maximum(m_prev, jnp.max(s, axis=1, keepdims=True))
        alpha = jnp.exp(m_prev - m_new)
        p = jnp.exp(s - m_new)
        l_scr[...] = alpha * l_scr[...] + jnp.sum(p, axis=1, keepdims=True)
        acc_scr[...] = alpha * acc_scr[...] + jnp.dot(p.astype(BF16), v_ref[0], preferred_element_type=F32)
        m_scr[...] = m_new

    @pl.when(ki < qi)
    def _():
        update(False)

    @pl.when(ki == qi)
    def _():
        update(True)
        o_ref[...] = acc_scr[...] / l_scr[...]


def _flash(q, k, v, batch, seq, tile):
    t = batch * seq
    nb = seq // tile
    q_spec = pl.BlockSpec((1, tile, HEAD_PAD), lambda b, h, qi, ki: (h, b * nb + qi, 0))
    kv_spec = pl.BlockSpec((1, tile, HEAD_PAD), lambda b, h, qi, ki: (h, b * nb + jnp.minimum(ki, qi), 0))
    return pl.pallas_call(
        functools.partial(_flash_kernel, tile=tile),
        grid=(batch, MLA_HEADS, nb, nb),
        in_specs=[q_spec, kv_spec, kv_spec],
        out_specs=pl.BlockSpec((tile, HEAD_PAD), lambda b, h, qi, ki: (b * nb + qi, h)),
        out_shape=jax.ShapeDtypeStruct((t, MLA_HEADS * HEAD_PAD), F32),
        scratch_shapes=[pltpu.VMEM((tile, 1), F32), pltpu.VMEM((tile, 1), F32),
                        pltpu.VMEM((tile, HEAD_PAD), F32)],
        compiler_params=_cparams("parallel", "parallel", "parallel", "arbitrary"),
        name="mla_flash",
    )(q, k, v)


def _ssm_tables(lam_re, lam_im, log_dt, b_re, b_im, c_re, c_im, d_skip):
    hp = lax.Precision.HIGHEST
    g, p, c = SSM_GROUPS, SSM_STATE, SSM_CHUNK
    dt = jnp.exp(log_dt)[:, None]
    mag = jnp.exp(lam_re * dt)
    ab_re, ab_im = mag * jnp.cos(lam_im * dt), mag * jnp.sin(lam_im * dt)
    den = lam_re * lam_re + lam_im * lam_im
    nr, ni = ab_re - 1.0, ab_im
    f_re = (nr * lam_re + ni * lam_im) / den
    f_im = (ni * lam_re - nr * lam_im) / den
    bb_re = f_re[..., None] * b_re - f_im[..., None] * b_im
    bb_im = f_re[..., None] * b_im + f_im[..., None] * b_re
    kk = jnp.arange(c + 1, dtype=F32)[:, None, None]
    pmag = jnp.exp(lam_re * dt * kk)
    pw_re, pw_im = pmag * jnp.cos(lam_im * dt * kk), pmag * jnp.sin(lam_im * dt * kk)
    ab_b_re = pw_re[:c, :, :, None] * bb_re - pw_im[:c, :, :, None] * bb_im
    ab_b_im = pw_re[:c, :, :, None] * bb_im + pw_im[:c, :, :, None] * bb_re
    kmat = (jnp.einsum('ghp,kgpj->gkhj', c_re, ab_b_re, precision=hp)
            - jnp.einsum('ghp,kgpj->gkhj', c_im, ab_b_im, precision=hp))
    kmat = kmat.at[:, 0].add(jax.vmap(jnp.diag)(d_skip))
    lag = jnp.arange(c)[None, :] - jnp.arange(c)[:, None]
    toe = kmat[:, jnp.clip(lag, 0, c - 1)]
    toe = jnp.where((lag >= 0)[None, :, :, None, None], toe, 0.0)
    toe = toe.transpose(0, 1, 4, 2, 3).reshape(g, CHUNK_W, CHUNK_W)
    rev_re, rev_im = ab_b_re[::-1], ab_b_im[::-1]
    st_re = rev_re.transpose(1, 0, 3, 2).reshape(g, CHUNK_W, p)
    st_im = rev_im.transpose(1, 0, 3, 2).reshape(g, CHUNK_W, p)
    odd = (jnp.arange(g) % 2 == 1)[:, None, None]

    def place(a):
        z = jnp.zeros_like(a)
        return jnp.where(odd, jnp.concatenate([z, a], -1), jnp.concatenate([a, z], -1))

    w1 = jnp.concatenate([toe, place(st_re), place(st_im)], axis=-1)
    ca_re = c_re[:, None] * pw_re[1:, :, None, :].transpose(1, 0, 2, 3) \
        - c_im[:, None] * pw_im[1:, :, None, :].transpose(1, 0, 2, 3)
    ca_im = c_re[:, None] * pw_im[1:, :, None, :].transpose(1, 0, 2, 3) \
        + c_im[:, None] * pw_re[1:, :, None, :].transpose(1, 0, 2, 3)
    q_re = ca_re.transpose(0, 3, 1, 2).reshape(g, p, CHUNK_W)
    q_im = -ca_im.transpose(0, 3, 1, 2).reshape(g, p, CHUNK_W)

    def place_rows(a):
        z = jnp.zeros_like(a)
        return jnp.where(odd, jnp.concatenate([z, a], 1), jnp.concatenate([a, z], 1))

    w3 = jnp.concatenate([place_rows(q_re), place_rows(q_im)], axis=1)
    a_re = pw_re[c].reshape(1, g * p)
    a_im = pw_im[c].reshape(1, g * p)
    return w1, w3, a_re, a_im


def _ssm_local_kernel(u_ref, w_ref, y_ref, lr_ref, li_ref):
    hp = lax.Precision.HIGHEST
    r0 = jnp.dot(u_ref[0, 0], w_ref[0], preferred_element_type=F32, precision=hp)
    r1 = jnp.dot(u_ref[0, 1], w_ref[1], preferred_element_type=F32, precision=hp)
    y_ref[0, 0] = r0[:, :CHUNK_W]
    y_ref[0, 1] = r1[:, :CHUNK_W]
    lr_ref[0] = r0[:, CHUNK_W:CHUNK_W + LANES] + r1[:, CHUNK_W:CHUNK_W + LANES]
    li_ref[0] = r0[:, CHUNK_W + LANES:] + r1[:, CHUNK_W + LANES:]


def _ssm_carry_kernel(lr_ref, li_ref, are_ref, aim_ref, xr_ref, xi_ref):
    a_re = are_ref[...]
    a_im = aim_ref[...]
    nc = lr_ref.shape[1]

    def step(c, carry):
        s_re, s_im = carry
        xr_ref[0, pl.ds(c, 1), :] = s_re
        xi_ref[0, pl.ds(c, 1), :] = s_im
        n_re = a_re * s_re - a_im * s_im + lr_ref[0, pl.ds(c, 1), :]
        n_im = a_re * s_im + a_im * s_re + li_ref[0, pl.ds(c, 1), :]
        return n_re, n_im

    zero = jnp.zeros((1, lr_ref.shape[2]), F32)
    lax.fori_loop(0, nc, step, (zero, zero))


def _ssm_out_kernel(y_ref, xr_ref, xi_ref, w_ref, o_ref):
    hp = lax.Precision.HIGHEST
    y = y_ref[0, 0]
    y = y + jnp.dot(xr_ref[0], w_ref[0, :LANES], preferred_element_type=F32, precision=hp)
    y = y + jnp.dot(xi_ref[0], w_ref[0, LANES:], preferred_element_type=F32, precision=hp)
    o_ref[0, 0] = _gelu_tanh(y)


def _ssm(u2d, batch, seq, tables):
    w1, w3, a_re, a_im = tables
    g, hh, c = SSM_GROUPS, SSM_GROUP, SSM_CHUNK
    nc = seq // c
    u4 = u2d.reshape(batch, nc, c, g, hh).transpose(0, 3, 1, 2, 4).reshape(batch, g, nc, CHUNK_W)
    blk = lambda b, gi: (b, gi, 0, 0)
    sw = g * SSM_STATE
    pair_blk = lambda b, qi: (b, qi, 0, 0)
    state_blk = lambda b, qi: (b, 0, qi)
    state_shape = jax.ShapeDtypeStruct((batch, nc, sw), F32)
    y_loc, l_re, l_im = pl.pallas_call(
        _ssm_local_kernel,
        grid=(batch, g // 2),
        in_specs=[pl.BlockSpec((1, 2, nc, CHUNK_W), pair_blk),
                  pl.BlockSpec((2, CHUNK_W, 2 * CHUNK_W), lambda b, qi: (qi, 0, 0))],
        out_specs=[pl.BlockSpec((1, 2, nc, CHUNK_W), pair_blk),
                   pl.BlockSpec((1, nc, LANES), state_blk), pl.BlockSpec((1, nc, LANES), state_blk)],
        out_shape=[jax.ShapeDtypeStruct((batch, g, nc, CHUNK_W), F32), state_shape, state_shape],
        compiler_params=_cparams("parallel", "parallel"),
        name="ssm_local",
    )(u4, w1)
    whole = lambda b: (b, 0, 0)
    x_re, x_im = pl.pallas_call(
        _ssm_carry_kernel,
        grid=(batch,),
        in_specs=[pl.BlockSpec((1, nc, sw), whole), pl.BlockSpec((1, nc, sw), whole),
                  pl.BlockSpec((1, sw), lambda b: (0, 0)),
                  pl.BlockSpec((1, sw), lambda b: (0, 0))],
        out_specs=[pl.BlockSpec((1, nc, sw), whole)] * 2,
        out_shape=[state_shape] * 2,
        compiler_params=_cparams("parallel"),
        name="ssm_carry",
    )(l_re, l_im, a_re, a_im)
    pair = lambda b, gi: (b, 0, gi // 2)
    y = pl.pallas_call(
        _ssm_out_kernel,
        grid=(batch, g),
        in_specs=[pl.BlockSpec((1, 1, nc, CHUNK_W), blk),
                  pl.BlockSpec((1, nc, LANES), pair),
                  pl.BlockSpec((1, nc, LANES), pair),
                  pl.BlockSpec((1, 2 * LANES, CHUNK_W), lambda b, gi: (gi, 0, 0))],
        out_specs=pl.BlockSpec((1, 1, nc, CHUNK_W), blk),
        out_shape=jax.ShapeDtypeStruct((batch, g, nc, CHUNK_W), F32),
        compiler_params=_cparams("parallel", "parallel"),
        name="ssm_out",
    )(y_loc, x_re, x_im, w3)
    return y.reshape(batch, g, nc, c, hh).transpose(0, 2, 3, 1, 4).reshape(batch * seq, SSM_WIDTH)


def _mix_out_kernel(x_ref, oa_ref, ys_ref, wglu_ref, bglu_ref, gm_ref, gs_ref, wa_ref, ws_ref,
                    g_ref, b_ref, o_ref):
    y = ys_ref[...]
    gate = jnp.dot(y.astype(BF16), wglu_ref[...], preferred_element_type=F32) + bglu_ref[...]
    y = y * jax.nn.sigmoid(gate)
    o_ssm = _rms(y, gs_ref[...], SSM_WIDTH).astype(BF16)
    o_att = _rms(oa_ref[...], gm_ref[...], MLA_WIDTH).astype(BF16)
    mixed = (jnp.dot(o_att, wa_ref[...], preferred_element_type=F32)
             + jnp.dot(o_ssm, ws_ref[...], preferred_element_type=F32))
    o_ref[...] = _layer_norm_rows(ALPHA * x_ref[...] + mixed, g_ref[...], b_ref[...])


def _mix_out(x2d, o_att, y_ssm, w_glu, b_glu, g_mla, g_ssm, w_out, ln_g, ln_b, tile):
    t = x2d.shape[0]
    hw = MLA_HEADS * HEAD_PAD
    pad = HEAD_PAD - V_HEAD
    g_mla_p = jnp.pad(g_mla.reshape(MLA_HEADS, V_HEAD), ((0, 0), (0, pad))).reshape(1, hw)
    wa = jnp.pad(w_out[:MLA_WIDTH].reshape(MLA_HEADS, V_HEAD, D_MODEL), ((0, 0), (0, pad), (0, 0)))
    wa = wa.reshape(hw, D_MODEL).astype(BF16)
    ws = w_out[MLA_WIDTH:].astype(BF16)
    row = lambda i: (i, 0)
    full = lambda i: (0, 0)
    return pl.pallas_call(
        _mix_out_kernel,
        grid=(t // tile,),
        in_specs=[
            pl.BlockSpec((tile, D_MODEL), row),
            pl.BlockSpec((tile, hw), row),
            pl.BlockSpec((tile, SSM_WIDTH), row),
            pl.BlockSpec((SSM_WIDTH, SSM_WIDTH), full),
            pl.BlockSpec((1, SSM_WIDTH), full),
            pl.BlockSpec((1, hw), full),
            pl.BlockSpec((1, SSM_WIDTH), full),
            pl.BlockSpec((hw, D_MODEL), full),
            pl.BlockSpec((SSM_WIDTH, D_MODEL), full),
            pl.BlockSpec((1, D_MODEL), full),
            pl.BlockSpec((1, D_MODEL), full),
        ],
        out_specs=pl.BlockSpec((tile, D_MODEL), row),
        out_shape=jax.ShapeDtypeStruct((t, D_MODEL), F32),
        compiler_params=_cparams("parallel"),
        name="mix_out",
    )(x2d, o_att, y_ssm, w_glu.astype(BF16), b_glu.reshape(1, -1), g_mla_p, g_ssm.reshape(1, -1),
      wa, ws, ln_g.reshape(1, -1), ln_b.reshape(1, -1))


def _mem_kv_kernel(mem_ref, wk_ref, wv_ref, k_ref, v_ref):
    m = mem_ref[...].astype(BF16)
    k_ref[...] = jnp.dot(m, wk_ref[...], preferred_element_type=F32).astype(BF16)
    v_ref[...] = jnp.dot(m, wv_ref[...], preferred_element_type=F32).astype(BF16)


def _mem_kv(mem2d, wk, wv):
    n = mem2d.shape[0]
    return pl.pallas_call(
        _mem_kv_kernel,
        out_shape=(jax.ShapeDtypeStruct((n, D_MODEL), BF16),) * 2,
        compiler_params=_cparams(),
        name="mem_kv",
    )(mem2d, wk.astype(BF16), wv.astype(BF16))


def _xattn_kernel(x_ref, k_ref, v_ref, wq_ref, wo_ref, g_ref, b_ref, o_ref):
    x = x_ref[...]
    q = jnp.dot(x.astype(BF16), wq_ref[...], preferred_element_type=F32) * (X_HEAD_DIM ** -0.5)
    q = q.astype(BF16)
    outs = []
    for h in range(X_HEADS):
        lo, hi = h * X_HEAD_DIM, (h + 1) * X_HEAD_DIM
        s = lax.dot_general(q[:, lo:hi], k_ref[0, :, lo:hi], (((1,), (1,)), ((), ())),
                            preferred_element_type=F32)
        p = jnp.exp(s - jnp.max(s, axis=1, keepdims=True))
        p = p / jnp.sum(p, axis=1, keepdims=True)
        outs.append(jnp.dot(p.astype(BF16), v_ref[0, :, lo:hi], preferred_element_type=F32))
    o = jnp.concatenate(outs, axis=1).astype(BF16)
    att = jnp.dot(o, wo_ref[...], preferred_element_type=F32)
    o_ref[...] = _layer_norm_rows(ALPHA * x + att, g_ref[...], b_ref[...])


def _xattn(x2d, kx, vx, wq, wo, ln_g, ln_b, seq, tile):
    t = x2d.shape[0]
    n_mem = kx.shape[1]
    per_b = seq // tile
    row = lambda i: (i, 0)
    full = lambda i: (0, 0)
    mem_spec = pl.BlockSpec((1, n_mem, D_MODEL), lambda i: (i // per_b, 0, 0))
    return pl.pallas_call(
        _xattn_kernel,
        grid=(t // tile,),
        in_specs=[pl.BlockSpec((tile, D_MODEL), row), mem_spec, mem_spec,
                  pl.BlockSpec((D_MODEL, D_MODEL), full), pl.BlockSpec((D_MODEL, D_MODEL), full),
                  pl.BlockSpec((1, D_MODEL), full), pl.BlockSpec((1, D_MODEL), full)],
        out_specs=pl.BlockSpec((tile, D_MODEL), row),
        out_shape=jax.ShapeDtypeStruct((t, D_MODEL), F32),
        compiler_params=_cparams("parallel"),
        name="mem_xattn",
    )(x2d, kx, vx, wq.astype(BF16), wo.astype(BF16), ln_g.reshape(1, -1), ln_b.reshape(1, -1))


def _extract_top(s, k):
    rows = lax.broadcasted_iota(jnp.int32, (k, s.shape[1]), 0)
    top = jnp.zeros((k, s.shape[1]), F32)
    for a in range(k):
        m = jnp.max(s, axis=0, keepdims=True)
        top = jnp.where(rows == a, m, top)
        s = jnp.where(s == m, NEG_INF, s)
    return top


def _peer_score_kernel(xt_ref, wq_ref, k1_ref, k2_ref, s1_ref, s2_ref, e1_ref, e2_ref, tau_ref):
    k = PEER_TOPK
    qt = jnp.dot(wq_ref[...], xt_ref[...], preferred_element_type=F32).astype(BF16)
    for h in range(PEER_HEADS):
        base = h * D_KEY
        s1 = jnp.dot(k1_ref[...], qt[base:base + HALF_KEY], preferred_element_type=F32)
        s2 = jnp.dot(k2_ref[...], qt[base + HALF_KEY:base + D_KEY], preferred_element_type=F32)
        v1 = _extract_top(s1, k)
        v2 = _extract_top(s2, k)
        cand = [v1[0:1] + v2]
        cand += [v1[a:a + 1] + v2[0:8] for a in range(1, 8)]
        cand += [v1[8:16] + v2[0:1]]
        top = _extract_top(jnp.concatenate(cand, axis=0), k)
        z = jnp.sum(jnp.exp(top - top[0:1]), axis=0, keepdims=True)
        s1_ref[h] = s1
        s2_ref[h] = s2
        e1_ref[h] = jnp.exp(s1 - v1[0:1]) / z
        e2_ref[h] = jnp.exp(s2 - v2[0:1])
        tau_ref[h] = top[k - 1:k]


def _peer_scores(xt_bf, w_query, keys1, keys2, tile):
    t = xt_bf.shape[1]
    wq_t = w_query.T.astype(BF16)
    col = lambda i: (0, 0, i)
    score_spec = pl.BlockSpec((PEER_HEADS, N_KEYS, tile), col)
    score_shape = jax.ShapeDtypeStruct((PEER_HEADS, N_KEYS, t), F32)
    return pl.pallas_call(
        _peer_score_kernel,
        grid=(t // tile,),
        in_specs=[pl.BlockSpec((D_MODEL, tile), lambda i: (0, i)),
                  pl.BlockSpec((PEER_HEADS * D_KEY, D_MODEL), lambda i: (0, 0)),
                  pl.BlockSpec((N_KEYS, HALF_KEY), lambda i: (0, 0)),
                  pl.BlockSpec((N_KEYS, HALF_KEY), lambda i: (0, 0))],
        out_specs=[score_spec] * 4 + [pl.BlockSpec((PEER_HEADS, 1, tile), col)],
        out_shape=[score_shape] * 4 + [jax.ShapeDtypeStruct((PEER_HEADS, 1, t), F32)],
        compiler_params=_cparams("parallel"),
        name="peer_scores",
    )(xt_bf, wq_t, keys1.astype(BF16), keys2.astype(BF16))


def _peer_main_kernel(xb_ref, x_ref, u_ref, vt_ref, s1_ref, s2_ref, e1_ref, e2_ref, tau_ref,
                      g_ref, b_ref, o_ref, acc_ref, w_ref, *, rows_per_step):
    ei = pl.program_id(1)

    @pl.when(ei == 0)
    def _():
        acc_ref[...] = jnp.zeros(acc_ref.shape, F32)

    xb = xb_ref[...]

    def row_block(il, carry):
        i = ei * rows_per_step + il
        r0 = pl.multiple_of(il * N_KEYS, N_KEYS)
        act = _gelu_tanh(jnp.dot(u_ref[pl.ds(r0, N_KEYS), :], xb, preferred_element_type=F32))
        gates = jnp.zeros(act.shape, F32)
        for h in range(PEER_HEADS):
            s = s1_ref[h, pl.ds(i, 1), :] + s2_ref[h]
            p = e1_ref[h, pl.ds(i, 1), :] * e2_ref[h]
            gates = gates + jnp.where(s >= tau_ref[h], p, 0.0)
        w_ref[pl.ds(r0, N_KEYS), :] = (gates * act).astype(BF16)
        return carry

    lax.fori_loop(0, rows_per_step, row_block, 0)
    acc_ref[...] += jnp.dot(vt_ref[...], w_ref[...], preferred_element_type=F32)

    @pl.when(ei == pl.num_programs(1) - 1)
    def _():
        z = ALPHA * x_ref[...] + acc_ref[...]
        mu = jnp.mean(z, axis=0, keepdims=True)
        zc = z - mu
        var = jnp.mean(zc * zc, axis=0, keepdims=True)
        o_ref[...] = zc * lax.rsqrt(var + LN_EPS) * g_ref[...] + b_ref[...]


def _peer_main(xt_bf, xt, u_bf, vt_bf, scores, ln_g, ln_b, tile, e_tile):
    t = xt.shape[1]
    s1, s2, e1, e2, tau = scores
    tok = lambda ti, ei: (0, ti)
    tok3 = lambda ti, ei: (0, 0, ti)
    score_spec = pl.BlockSpec((PEER_HEADS, N_KEYS, tile), tok3)
    return pl.pallas_call(
        functools.partial(_peer_main_kernel, rows_per_step=e_tile // N_KEYS),
        grid=(t // tile, N_EXPERTS // e_tile),
        in_specs=[pl.BlockSpec((D_MODEL, tile), tok),
                  pl.BlockSpec((D_MODEL, tile), tok),
                  pl.BlockSpec((e_tile, D_MODEL), lambda ti, ei: (ei, 0)),
                  pl.BlockSpec((D_MODEL, e_tile), lambda ti, ei: (0, ei)),
                  score_spec, score_spec, score_spec, score_spec,
                  pl.BlockSpec((PEER_HEADS, 1, tile), tok3),
                  pl.BlockSpec((D_MODEL, 1), lambda ti, ei: (0, 0)),
                  pl.BlockSpec((D_MODEL, 1), lambda ti, ei: (0, 0))],
        out_specs=pl.BlockSpec((D_MODEL, tile), tok),
        out_shape=jax.ShapeDtypeStruct((D_MODEL, t), F32),
        scratch_shapes=[pltpu.VMEM((D_MODEL, tile), F32), pltpu.VMEM((e_tile, tile), BF16)],
        compiler_params=_cparams("parallel", "arbitrary"),
        name="peer_main",
    )(xt_bf, xt, u_bf, vt_bf, s1, s2, e1, e2, tau, ln_g.reshape(-1, 1), ln_b.reshape(-1, 1))


def _layer(x2d, mem2d, cos_t, sin_t, batch, seq, p):
    tile = min(512, seq)
    q, k, v, u = _in_proj(x2d, cos_t, sin_t, p["w_in"], p["g_cq"], p["w_uq"], p["g_ckv"], p["w_ukv"], tile)
    o_att = _flash(q, k, v, batch, seq, tile)
    tables = _ssm_tables(p["lam_re"], p["lam_im"], p["log_dt"], p["b_re"], p["b_im"], p["c_re"],
                         p["c_im"], p["d_skip"])
    y_ssm = _ssm(u, batch, seq, tables)
    x1 = _mix_out(x2d, o_att, y_ssm, p["w_glu"], p["b_glu"], p["g_mla_out"], p["g_ssm_out"], p["w_out"],
                  p["ln1_g"], p["ln1_b"], tile)
    kx, vx = _mem_kv(mem2d, p["xk"], p["xv"])
    n_mem = mem2d.shape[0] // batch
    x2 = _xattn(x1, kx.reshape(batch, n_mem, D_MODEL), vx.reshape(batch, n_mem, D_MODEL), p["xq"], p["xo"],
                p["ln2_g"], p["ln2_b"], seq, tile)
    x2t = x2.T
    x2t_bf = x2t.astype(BF16)
    scores = _peer_scores(x2t_bf, p["w_query"], p["sub_keys1"], p["sub_keys2"], min(256, seq))
    out_t = _peer_main(x2t_bf, x2t, p["u_tab"].astype(BF16), p["v_tab"].T.astype(BF16), scores,
                       p["ln3_g"], p["ln3_b"], tile, 1024)
    return out_t.T


def kernel(x, mem, positions, w_in, g_cq, w_uq, g_ckv, w_ukv, lam_re, lam_im, log_dt, b_re, b_im, c_re, c_im, d_skip, w_glu, b_glu, g_mla_out, g_ssm_out, w_out, ln1_g, ln1_b, xq, xk, xv, xo, ln2_g, ln2_b, w_query, sub_keys1, sub_keys2, u_tab, v_tab, ln3_g, ln3_b):
    batch, seq, _ = x.shape
    params = dict(w_in=w_in, g_cq=g_cq, w_uq=w_uq, g_ckv=g_ckv, w_ukv=w_ukv, lam_re=lam_re, lam_im=lam_im,
                  log_dt=log_dt, b_re=b_re, b_im=b_im, c_re=c_re, c_im=c_im, d_skip=d_skip, w_glu=w_glu,
                  b_glu=b_glu, g_mla_out=g_mla_out, g_ssm_out=g_ssm_out, w_out=w_out, ln1_g=ln1_g,
                  ln1_b=ln1_b, xq=xq, xk=xk, xv=xv, xo=xo, ln2_g=ln2_g, ln2_b=ln2_b, w_query=w_query,
                  sub_keys1=sub_keys1, sub_keys2=sub_keys2, u_tab=u_tab, v_tab=v_tab, ln3_g=ln3_g,
                  ln3_b=ln3_b)
    cos_t, sin_t = _rope_tables(positions)
    h = x.reshape(batch * seq, D_MODEL)
    mem2d = mem.reshape(-1, D_MODEL)
    for l in range(w_in.shape[0]):
        h = _layer(h, mem2d, cos_t, sin_t, batch, seq, {k: v[l] for k, v in params.items()})
    return h.reshape(batch, seq, D_MODEL)
```

```python
import functools
import math

import jax
import jax.numpy as jnp
from jax import lax
from jax.experimental import pallas as pl
from jax.experimental.pallas import tpu as pltpu

F32 = jnp.float32
BF16 = jnp.bfloat16

D_MODEL = 1024
MLA_HEADS = 8
QK_NOPE = 64
QK_ROPE = 32
V_HEAD = 64
Q_LORA = 384
KV_LORA = 256
MLA_WIDTH = MLA_HEADS * V_HEAD
SSM_WIDTH = 512
SSM_GROUP = 16
SSM_GROUPS = SSM_WIDTH // SSM_GROUP
SSM_STATE = 64
X_HEADS = 4
X_HEAD_DIM = D_MODEL // X_HEADS
N_KEYS = 128
N_EXPERTS = N_KEYS * N_KEYS
PEER_HEADS = 8
PEER_TOPK = 16
D_KEY = 256
HALF_KEY = D_KEY // 2
ROPE_THETA = 10000.0
LN_EPS = 1e-5
RMS_EPS = 1e-6
DEPTH = 1
ALPHA = (2 * DEPTH) ** 0.25

LANES = 128
HEAD_PAD = 128
SSM_CHUNK = 16
CHUNK_W = SSM_CHUNK * SSM_GROUP
VMEM_LIMIT = 56 * 1024 * 1024

NEG_INF = float("-inf")


def _cparams(*sem):
    return pltpu.CompilerParams(dimension_semantics=sem, vmem_limit_bytes=VMEM_LIMIT)


def _gelu_tanh(x):
    c = math.sqrt(2.0 / math.pi)
    return 0.5 * x * (1.0 + jnp.tanh(c * (x + 0.044715 * (x * x * x))))


def _rms(x, g, n):
    ms = jnp.sum(x * x, axis=-1, keepdims=True) * (1.0 / n)
    return x * lax.rsqrt(ms + RMS_EPS) * g


def _layer_norm_rows(z, g, b):
    mu = jnp.mean(z, axis=-1, keepdims=True)
    zc = z - mu
    var = jnp.mean(zc * zc, axis=-1, keepdims=True)
    return zc * lax.rsqrt(var + LN_EPS) * g + b


def _rope_kernel(pos_ref, cos_ref, sin_ref):
    half = QK_ROPE // 2
    lane = lax.broadcasted_iota(jnp.int32, pos_ref.shape, 1)
    k = (lane % half).astype(F32)
    freq = jnp.exp(k * (-math.log(ROPE_THETA) / half))
    ang = pos_ref[...].astype(F32) * freq
    cos_ref[...] = jnp.cos(ang)
    sin_ref[...] = jnp.sin(ang)


def _rope_tables(positions):
    half = QK_ROPE // 2
    t = positions.size
    per_row = LANES // half
    pos = jnp.repeat(positions.reshape(t // per_row, per_row), half, axis=1)
    cos, sin = pl.pallas_call(
        _rope_kernel,
        out_shape=(jax.ShapeDtypeStruct(pos.shape, F32),) * 2,
        name="rope_tables",
    )(pos)
    cos = cos.reshape(t, half)
    sin = sin.reshape(t, half)
    pad = jnp.zeros((t, HEAD_PAD - QK_NOPE - QK_ROPE), F32)
    cos_t = jnp.concatenate([jnp.ones((t, QK_NOPE), F32), cos, cos, pad], axis=1)
    sin_t = jnp.concatenate([jnp.zeros((t, QK_NOPE), F32), sin, sin, pad], axis=1)
    return cos_t, sin_t


def _in_proj_kernel(x_ref, cos_ref, sin_ref, w_in_ref, gq_ref, wq_ref, gkv_ref, wkv_ref,
                    q_ref, k_ref, v_ref, u_ref):
    hw = MLA_HEADS * HEAD_PAD
    proj = jnp.dot(x_ref[...].astype(BF16), w_in_ref[...], preferred_element_type=F32)
    cos = cos_ref[...]
    sin = sin_ref[...]
    o_kv = Q_LORA
    o_u = Q_LORA + KV_LORA
    o_kr = o_u + SSM_WIDTH
    cq = _rms(proj[:, :Q_LORA], gq_ref[...], Q_LORA).astype(BF16)
    ckv = _rms(proj[:, o_kv:o_u], gkv_ref[...], KV_LORA).astype(BF16)
    u_ref[...] = proj[:, o_u:o_kr]
    k_rope = proj[:, o_kr:o_kr + HEAD_PAD] * cos + proj[:, o_kr + HEAD_PAD:o_kr + 2 * HEAD_PAD] * sin
    qx = jnp.dot(cq, wq_ref[...], preferred_element_type=F32)
    kvx = jnp.dot(ckv, wkv_ref[...], preferred_element_type=F32)
    scale = (QK_NOPE + QK_ROPE) ** -0.5 * math.log2(math.e)
    ones_lane = lax.broadcasted_iota(jnp.int32, cos.shape, 1) == V_HEAD
    for h in range(MLA_HEADS):
        lo, hi = h * HEAD_PAD, (h + 1) * HEAD_PAD
        q = qx[:, lo:hi] * cos + qx[:, hw + lo:hw + hi] * sin
        q_ref[h] = (q * scale).astype(BF16)
        k_ref[h] = (kvx[:, lo:hi] + k_rope).astype(BF16)
        v_ref[h] = jnp.where(ones_lane, 1.0, kvx[:, hw + lo:hw + hi]).astype(BF16)


def _pad_heads(w, width, lo, hi):
    n = w.shape[0]
    w = w.reshape(n, MLA_HEADS, width)[:, :, lo:hi]
    w = jnp.pad(w, ((0, 0), (0, 0), (0, HEAD_PAD - (hi - lo))))
    return w.reshape(n, MLA_HEADS * HEAD_PAD)


def _rope_partner(w, off):
    half = QK_ROPE // 2
    x1 = w[..., off:off + half]
    x2 = w[..., off + half:off + 2 * half]
    return x1, x2, jnp.concatenate([-x2, x1], axis=-1)


def _in_proj(x2d, cos_t, sin_t, w_in, g_cq, w_uq, g_ckv, w_ukv, tile):
    t = x2d.shape[0]
    dq = QK_NOPE + QK_ROPE
    o_kr = Q_LORA + KV_LORA
    w_kr = w_in[:, o_kr:o_kr + QK_ROPE]
    _, _, kr_sw = _rope_partner(w_kr, 0)
    lane_pad = ((0, 0), (QK_NOPE, HEAD_PAD - dq))
    w_in_ext = jnp.concatenate(
        [w_in[:, :o_kr], w_in[:, o_kr + QK_ROPE:], jnp.pad(w_kr, lane_pad), jnp.pad(kr_sw, lane_pad)],
        axis=1).astype(BF16)
    wq3 = w_uq.reshape(Q_LORA, MLA_HEADS, dq)
    _, _, q_sw = _rope_partner(wq3, QK_NOPE)
    q_sw = jnp.pad(q_sw, ((0, 0), (0, 0), (QK_NOPE, HEAD_PAD - dq))).reshape(Q_LORA, MLA_HEADS * HEAD_PAD)
    wq = jnp.concatenate([_pad_heads(w_uq, dq, 0, dq), q_sw], axis=1).astype(BF16)
    wkv = jnp.concatenate([_pad_heads(w_ukv, QK_NOPE + V_HEAD, 0, QK_NOPE),
                           _pad_heads(w_ukv, QK_NOPE + V_HEAD, QK_NOPE, QK_NOPE + V_HEAD)],
                          axis=1).astype(BF16)
    n_ext = w_in_ext.shape[1]
    hw = MLA_HEADS * HEAD_PAD
    row = lambda i: (i, 0)
    full = lambda i: (0, 0)
    head_spec = pl.BlockSpec((MLA_HEADS, tile, HEAD_PAD), lambda i: (0, i, 0))
    head_shape = jax.ShapeDtypeStruct((MLA_HEADS, t, HEAD_PAD), BF16)
    return pl.pallas_call(
        _in_proj_kernel,
        grid=(t // tile,),
        in_specs=[
            pl.BlockSpec((tile, D_MODEL), row),
            pl.BlockSpec((tile, HEAD_PAD), row),
            pl.BlockSpec((tile, HEAD_PAD), row),
            pl.BlockSpec((D_MODEL, n_ext), full),
            pl.BlockSpec((1, Q_LORA), full),
            pl.BlockSpec((Q_LORA, 2 * hw), full),
            pl.BlockSpec((1, KV_LORA), full),
            pl.BlockSpec((KV_LORA, 2 * hw), full),
        ],
        out_specs=[head_spec, head_spec, head_spec, pl.BlockSpec((tile, SSM_WIDTH), row)],
        out_shape=[head_shape, head_shape, head_shape, jax.ShapeDtypeStruct((t, SSM_WIDTH), F32)],
        compiler_params=_cparams("parallel"),
        name="in_proj",
    )(x2d, cos_t, sin_t, w_in_ext, g_cq.reshape(1, -1), wq, g_ckv.reshape(1, -1), wkv)


def _flash_kernel(q_ref, k_ref, v_ref, o_ref, m_scr, acc_scr, *, tq, tk):
    qi = pl.program_id(2)
    per_q = tq // tk
    m_scr[...] = jnp.full(m_scr.shape, NEG_INF, F32)
    acc_scr[...] = jnp.zeros(acc_scr.shape, F32)

    def block(kb, r_lo, masked):
        k0 = pl.multiple_of(kb * tk, tk)
        s = lax.dot_general(q_ref[0, r_lo:tq, :], k_ref[0, pl.ds(k0, tk), :], (((1,), (1,)), ((), ())),
                            preferred_element_type=F32)
        if masked:
            row = lax.broadcasted_iota(jnp.int32, s.shape, 0)
            col = lax.broadcasted_iota(jnp.int32, s.shape, 1)
            s = jnp.where(col <= row, s, NEG_INF)
        m_prev = m_scr[r_lo:tq, :]
        m_new = jnp.maximum(m_prev, jnp.max(s, axis=1, keepdims=True))
        alpha = jnp.exp2(m_prev - m_new)
        p = jnp.exp2(s - jnp.tile(m_new, (1, tk // LANES)))
        pv = jnp.dot(p.astype(BF16), v_ref[0, pl.ds(k0, tk), :], preferred_element_type=F32)
        acc_scr[r_lo:tq, :] = alpha * acc_scr[r_lo:tq, :] + pv
        m_scr[r_lo:tq, :] = m_new

    def body(kb, carry):
        block(kb, 0, False)
        return carry

    lax.fori_loop(0, qi * per_q, body, 0)
    for d in range(per_q):
        block(qi * per_q + d, d * tk, True)
    acc = acc_scr[...]
    lane = lax.broadcasted_iota(jnp.int32, acc.shape, 1)
    o_ref[...] = jnp.where(lane < V_HEAD, acc / acc[:, V_HEAD:V_HEAD + 1], 0.0)


def _flash(q, k, v, batch, seq, tq, tk):
    t = batch * seq
    nb = seq // tq
    q_spec = pl.BlockSpec((1, tq, HEAD_PAD), lambda b, h, qi: (h, b * nb + qi, 0))
    kv_spec = pl.BlockSpec((1, seq, HEAD_PAD), lambda b, h, qi: (h, b, 0))
    return pl.pallas_call(
        functools.partial(_flash_kernel, tq=tq, tk=tk),
        grid=(batch, MLA_HEADS, nb),
        in_specs=[q_spec, kv_spec, kv_spec],
        out_specs=pl.BlockSpec((tq, HEAD_PAD), lambda b, h, qi: (b * nb + qi, h)),
        out_shape=jax.ShapeDtypeStruct((t, MLA_HEADS * HEAD_PAD), F32),
        scratch_shapes=[pltpu.VMEM((tq, LANES), F32), pltpu.VMEM((tq, HEAD_PAD), F32)],
        compiler_params=_cparams("parallel", "parallel", "arbitrary"),
        name="mla_flash",
    )(q, k, v)


def _ssm_tables(lam_re, lam_im, log_dt, b_re, b_im, c_re, c_im, d_skip):
    hp = lax.Precision.HIGHEST
    g, p, c = SSM_GROUPS, SSM_STATE, SSM_CHUNK
    dt = jnp.exp(log_dt)[:, None]
    mag = jnp.exp(lam_re * dt)
    ab_re, ab_im = mag * jnp.cos(lam_im * dt), mag * jnp.sin(lam_im * dt)
    den = lam_re * lam_re + lam_im * lam_im
    nr, ni = ab_re - 1.0, ab_im
    f_re = (nr * lam_re + ni * lam_im) / den
    f_im = (ni * lam_re - nr * lam_im) / den
    bb_re = f_re[..., None] * b_re - f_im[..., None] * b_im
    bb_im = f_re[..., None] * b_im + f_im[..., None] * b_re
    kk = jnp.arange(c + 1, dtype=F32)[:, None, None]
    pmag = jnp.exp(lam_re * dt * kk)
    pw_re, pw_im = pmag * jnp.cos(lam_im * dt * kk), pmag * jnp.sin(lam_im * dt * kk)
    ab_b_re = pw_re[:c, :, :, None] * bb_re - pw_im[:c, :, :, None] * bb_im
    ab_b_im = pw_re[:c, :, :, None] * bb_im + pw_im[:c, :, :, None] * bb_re
    kmat = (jnp.einsum('ghp,kgpj->gkhj', c_re, ab_b_re, precision=hp)
            - jnp.einsum('ghp,kgpj->gkhj', c_im, ab_b_im, precision=hp))
    kmat = kmat.at[:, 0].add(jax.vmap(jnp.diag)(d_skip))
    lag = jnp.arange(c)[None, :] - jnp.arange(c)[:, None]
    toe = kmat[:, jnp.clip(lag, 0, c - 1)]
    toe = jnp.where((lag >= 0)[None, :, :, None, None], toe, 0.0)
    toe = toe.transpose(0, 1, 4, 2, 3).reshape(g, CHUNK_W, CHUNK_W)
    rev_re, rev_im = ab_b_re[::-1], ab_b_im[::-1]
    st_re = rev_re.transpose(1, 0, 3, 2).reshape(g, CHUNK_W, p)
    st_im = rev_im.transpose(1, 0, 3, 2).reshape(g, CHUNK_W, p)
    odd = (jnp.arange(g) % 2 == 1)[:, None, None]

    def place(a):
        z = jnp.zeros_like(a)
        return jnp.where(odd, jnp.concatenate([z, a], -1), jnp.concatenate([a, z], -1))

    w1 = jnp.concatenate([toe, place(st_re), place(st_im)], axis=-1)
    ca_re = c_re[:, None] * pw_re[1:, :, None, :].transpose(1, 0, 2, 3) \
        - c_im[:, None] * pw_im[1:, :, None, :].transpose(1, 0, 2, 3)
    ca_im = c_re[:, None] * pw_im[1:, :, None, :].transpose(1, 0, 2, 3) \
        + c_im[:, None] * pw_re[1:, :, None, :].transpose(1, 0, 2, 3)
    q_re = ca_re.transpose(0, 3, 1, 2).reshape(g, p, CHUNK_W)
    q_im = -ca_im.transpose(0, 3, 1, 2).reshape(g, p, CHUNK_W)

    def place_rows(a):
        z = jnp.zeros_like(a)
        return jnp.where(odd, jnp.concatenate([z, a], 1), jnp.concatenate([a, z], 1))

    w3 = jnp.concatenate([place_rows(q_re), place_rows(q_im)], axis=1)
    a_re = pw_re[c].reshape(1, g * p)
    a_im = pw_im[c].reshape(1, g * p)
    return w1, w3, a_re, a_im


def _ssm_local_kernel(u_ref, w_ref, y_ref, lr_ref, li_ref):
    hp = lax.Precision.HIGHEST
    r0 = jnp.dot(u_ref[0, 0], w_ref[0], preferred_element_type=F32, precision=hp)
    r1 = jnp.dot(u_ref[0, 1], w_ref[1], preferred_element_type=F32, precision=hp)
    y_ref[0, 0] = r0[:, :CHUNK_W]
    y_ref[0, 1] = r1[:, :CHUNK_W]
    lr_ref[0] = r0[:, CHUNK_W:CHUNK_W + LANES] + r1[:, CHUNK_W:CHUNK_W + LANES]
    li_ref[0] = r0[:, CHUNK_W + LANES:] + r1[:, CHUNK_W + LANES:]


def _ssm_carry_kernel(lr_ref, li_ref, are_ref, aim_ref, xr_ref, xi_ref):
    a_re = are_ref[...]
    a_im = aim_ref[...]
    nc = lr_ref.shape[1]

    def step(c, carry):
        s_re, s_im = carry
        xr_ref[0, pl.ds(c, 1), :] = s_re
        xi_ref[0, pl.ds(c, 1), :] = s_im
        n_re = a_re * s_re - a_im * s_im + lr_ref[0, pl.ds(c, 1), :]
        n_im = a_re * s_im + a_im * s_re + li_ref[0, pl.ds(c, 1), :]
        return n_re, n_im

    zero = jnp.zeros((1, lr_ref.shape[2]), F32)
    lax.fori_loop(0, nc, step, (zero, zero))


def _ssm_out_kernel(y_ref, xr_ref, xi_ref, w_ref, o_ref):
    hp = lax.Precision.HIGHEST
    y = y_ref[0, 0]
    y = y + jnp.dot(xr_ref[0], w_ref[0, :LANES], preferred_element_type=F32, precision=hp)
    y = y + jnp.dot(xi_ref[0], w_ref[0, LANES:], preferred_element_type=F32, precision=hp)
    o_ref[0, 0] = _gelu_tanh(y)


def _ssm(u2d, batch, seq, tables):
    w1, w3, a_re, a_im = tables
    g, hh, c = SSM_GROUPS, SSM_GROUP, SSM_CHUNK
    nc = seq // c
    u4 = u2d.reshape(batch, nc, c, g, hh).transpose(0, 3, 1, 2, 4).reshape(batch, g, nc, CHUNK_W)
    blk = lambda b, gi: (b, gi, 0, 0)
    sw = g * SSM_STATE
    pair_blk = lambda b, qi: (b, qi, 0, 0)
    state_blk = lambda b, qi: (b, 0, qi)
    state_shape = jax.ShapeDtypeStruct((batch, nc, sw), F32)
    y_loc, l_re, l_im = pl.pallas_call(
        _ssm_local_kernel,
        grid=(batch, g // 2),
        in_specs=[pl.BlockSpec((1, 2, nc, CHUNK_W), pair_blk),
                  pl.BlockSpec((2, CHUNK_W, 2 * CHUNK_W), lambda b, qi: (qi, 0, 0))],
        out_specs=[pl.BlockSpec((1, 2, nc, CHUNK_W), pair_blk),
                   pl.BlockSpec((1, nc, LANES), state_blk), pl.BlockSpec((1, nc, LANES), state_blk)],
        out_shape=[jax.ShapeDtypeStruct((batch, g, nc, CHUNK_W), F32), state_shape, state_shape],
        compiler_params=_cparams("parallel", "parallel"),
        name="ssm_local",
    )(u4, w1)
    whole = lambda b: (b, 0, 0)
    x_re, x_im = pl.pallas_call(
        _ssm_carry_kernel,
        grid=(batch,),
        in_specs=[pl.BlockSpec((1, nc, sw), whole), pl.BlockSpec((1, nc, sw), whole),
                  pl.BlockSpec((1, sw), lambda b: (0, 0)),
                  pl.BlockSpec((1, sw), lambda b: (0, 0))],
        out_specs=[pl.BlockSpec((1, nc, sw), whole)] * 2,
        out_shape=[state_shape] * 2,
        compiler_params=_cparams("parallel"),
        name="ssm_carry",
    )(l_re, l_im, a_re, a_im)
    pair = lambda b, gi: (b, 0, gi // 2)
    y = pl.pallas_call(
        _ssm_out_kernel,
        grid=(batch, g),
        in_specs=[pl.BlockSpec((1, 1, nc, CHUNK_W), blk),
                  pl.BlockSpec((1, nc, LANES), pair),
                  pl.BlockSpec((1, nc, LANES), pair),
                  pl.BlockSpec((1, 2 * LANES, CHUNK_W), lambda b, gi: (gi, 0, 0))],
        out_specs=pl.BlockSpec((1, 1, nc, CHUNK_W), blk),
        out_shape=jax.ShapeDtypeStruct((batch, g, nc, CHUNK_W), F32),
        compiler_params=_cparams("parallel", "parallel"),
        name="ssm_out",
    )(y_loc, x_re, x_im, w3)
    return y.reshape(batch, g, nc, c, hh).transpose(0, 2, 3, 1, 4).reshape(batch * seq, SSM_WIDTH)


def _mix_out_kernel(x_ref, oa_ref, ys_ref, wglu_ref, bglu_ref, gm_ref, gs_ref, wa_ref, ws_ref,
                    g_ref, b_ref, o_ref):
    y = ys_ref[...]
    gate = jnp.dot(y.astype(BF16), wglu_ref[...], preferred_element_type=F32) + bglu_ref[...]
    y = y * jax.nn.sigmoid(gate)
    o_ssm = _rms(y, gs_ref[...], SSM_WIDTH).astype(BF16)
    o_att = _rms(oa_ref[...], gm_ref[...], MLA_WIDTH).astype(BF16)
    mixed = (jnp.dot(o_att, wa_ref[...], preferred_element_type=F32)
             + jnp.dot(o_ssm, ws_ref[...], preferred_element_type=F32))
    o_ref[...] = _layer_norm_rows(ALPHA * x_ref[...] + mixed, g_ref[...], b_ref[...])


def _mix_out(x2d, o_att, y_ssm, w_glu, b_glu, g_mla, g_ssm, w_out, ln_g, ln_b, tile):
    t = x2d.shape[0]
    hw = MLA_HEADS * HEAD_PAD
    pad = HEAD_PAD - V_HEAD
    g_mla_p = jnp.pad(g_mla.reshape(MLA_HEADS, V_HEAD), ((0, 0), (0, pad))).reshape(1, hw)
    wa = jnp.pad(w_out[:MLA_WIDTH].reshape(MLA_HEADS, V_HEAD, D_MODEL), ((0, 0), (0, pad), (0, 0)))
    wa = wa.reshape(hw, D_MODEL).astype(BF16)
    ws = w_out[MLA_WIDTH:].astype(BF16)
    row = lambda i: (i, 0)
    full = lambda i: (0, 0)
    return pl.pallas_call(
        _mix_out_kernel,
        grid=(t // tile,),
        in_specs=[
            pl.BlockSpec((tile, D_MODEL), row),
            pl.BlockSpec((tile, hw), row),
            pl.BlockSpec((tile, SSM_WIDTH), row),
            pl.BlockSpec((SSM_WIDTH, SSM_WIDTH), full),
            pl.BlockSpec((1, SSM_WIDTH), full),
            pl.BlockSpec((1, hw), full),
            pl.BlockSpec((1, SSM_WIDTH), full),
            pl.BlockSpec((hw, D_MODEL), full),
            pl.BlockSpec((SSM_WIDTH, D_MODEL), full),
            pl.BlockSpec((1, D_MODEL), full),
            pl.BlockSpec((1, D_MODEL), full),
        ],
        out_specs=pl.BlockSpec((tile, D_MODEL), row),
        out_shape=jax.ShapeDtypeStruct((t, D_MODEL), F32),
        compiler_params=_cparams("parallel"),
        name="mix_out",
    )(x2d, o_att, y_ssm, w_glu.astype(BF16), b_glu.reshape(1, -1), g_mla_p, g_ssm.reshape(1, -1),
      wa, ws, ln_g.reshape(1, -1), ln_b.reshape(1, -1))


def _mem_kv_kernel(mem_ref, wk_ref, wv_ref, k_ref, v_ref):
    m = mem_ref[...].astype(BF16)
    k_ref[...] = jnp.dot(m, wk_ref[...], preferred_element_type=F32).astype(BF16)
    v_ref[...] = jnp.dot(m, wv_ref[...], preferred_element_type=F32).astype(BF16)


def _mem_kv(mem2d, wk, wv):
    n = mem2d.shape[0]
    return pl.pallas_call(
        _mem_kv_kernel,
        out_shape=(jax.ShapeDtypeStruct((n, D_MODEL), BF16),) * 2,
        compiler_params=_cparams(),
        name="mem_kv",
    )(mem2d, wk.astype(BF16), wv.astype(BF16))


def _xattn_kernel(x_ref, k_ref, v_ref, wq_ref, wo_ref, g_ref, b_ref, o_ref):
    x = x_ref[...]
    q = jnp.dot(x.astype(BF16), wq_ref[...], preferred_element_type=F32) * (X_HEAD_DIM ** -0.5)
    q = q.astype(BF16)
    outs = []
    for h in range(X_HEADS):
        lo, hi = h * X_HEAD_DIM, (h + 1) * X_HEAD_DIM
        s = lax.dot_general(q[:, lo:hi], k_ref[0, :, lo:hi], (((1,), (1,)), ((), ())),
                            preferred_element_type=F32)
        p = jnp.exp(s - jnp.max(s, axis=1, keepdims=True))
        p = p / jnp.sum(p, axis=1, keepdims=True)
        outs.append(jnp.dot(p.astype(BF16), v_ref[0, :, lo:hi], preferred_element_type=F32))
    o = jnp.concatenate(outs, axis=1).astype(BF16)
    att = jnp.dot(o, wo_ref[...], preferred_element_type=F32)
    o_ref[...] = _layer_norm_rows(ALPHA * x + att, g_ref[...], b_ref[...])


def _xattn(x2d, kx, vx, wq, wo, ln_g, ln_b, seq, tile):
    t = x2d.shape[0]
    n_mem = kx.shape[1]
    per_b = seq // tile
    row = lambda i: (i, 0)
    full = lambda i: (0, 0)
    mem_spec = pl.BlockSpec((1, n_mem, D_MODEL), lambda i: (i // per_b, 0, 0))
    return pl.pallas_call(
        _xattn_kernel,
        grid=(t // tile,),
        in_specs=[pl.BlockSpec((tile, D_MODEL), row), mem_spec, mem_spec,
                  pl.BlockSpec((D_MODEL, D_MODEL), full), pl.BlockSpec((D_MODEL, D_MODEL), full),
                  pl.BlockSpec((1, D_MODEL), full), pl.BlockSpec((1, D_MODEL), full)],
        out_specs=pl.BlockSpec((tile, D_MODEL), row),
        out_shape=jax.ShapeDtypeStruct((t, D_MODEL), F32),
        compiler_params=_cparams("parallel"),
        name="mem_xattn",
    )(x2d, kx, vx, wq.astype(BF16), wo.astype(BF16), ln_g.reshape(1, -1), ln_b.reshape(1, -1))


def _extract_top(s, k, want_rank=False):
    rows = lax.broadcasted_iota(jnp.int32, (k, s.shape[1]), 0)
    top = jnp.zeros((k, s.shape[1]), F32)
    rank = jnp.full(s.shape, float(k), F32) if want_rank else None
    for a in range(k):
        m = jnp.max(s, axis=0, keepdims=True)
        top = jnp.where(rows == a, m, top)
        hit = s == m
        if want_rank:
            rank = jnp.where(hit, float(a), rank)
        s = jnp.where(hit, NEG_INF, s)
    return top, rank


def _peer_score_kernel(xt_ref, wq_ref, k1_ref, k2_ref, e1_ref, n1_ref, e2_ref, r2_ref):
    k = PEER_TOPK
    qt = jnp.dot(wq_ref[...], xt_ref[...], preferred_element_type=F32).astype(BF16)
    for h in range(PEER_HEADS):
        base = h * D_KEY
        s1 = jnp.dot(k1_ref[...], qt[base:base + HALF_KEY], preferred_element_type=F32)
        s2 = jnp.dot(k2_ref[...], qt[base + HALF_KEY:base + D_KEY], preferred_element_type=F32)
        v1, _ = _extract_top(s1, k)
        v2, r2 = _extract_top(s2, k, want_rank=True)
        cand = [v1[0:1] + v2]
        cand += [v1[a:a + 1] + v2[0:8] for a in range(1, 8)]
        cand += [v1[8:16] + v2[0:1]]
        top, _ = _extract_top(jnp.concatenate(cand, axis=0), k)
        tau = top[k - 1:k]
        z = jnp.sum(jnp.exp(top - top[0:1]), axis=0, keepdims=True)
        n_a = jnp.zeros(v1.shape, F32)
        for b in range(k):
            n_a = n_a + jnp.where(v1 + v2[b:b + 1] >= tau, 1.0, 0.0)
        n1 = jnp.zeros(s1.shape, F32)
        for a in range(k):
            n1 = jnp.where(s1 == v1[a:a + 1], n_a[a:a + 1], n1)
        e1 = _bf16_pair(jnp.exp(s1 - v1[0:1]) / z)
        n1 = _bf16_pair(n1)
        e2 = jnp.exp(s2 - v2[0:1]).astype(BF16)
        r2 = r2.astype(BF16)
        for c in range(s1.shape[1] // LANES):
            cols = slice(c * LANES, (c + 1) * LANES)
            e1_ref[h, c] = e1[:, cols]
            n1_ref[h, c] = n1[:, cols]
            e2_ref[h, c] = pltpu.bitcast(e2[:, cols], jnp.uint32)
            r2_ref[h, c] = pltpu.bitcast(r2[:, cols], jnp.uint32)


def _bf16_pair(x):
    u = pltpu.bitcast(x.astype(BF16).astype(F32), jnp.uint32)
    return u | (u >> 16)


def _peer_scores(xt_bf, w_query, keys1, keys2, tile):
    t = xt_bf.shape[1]
    wq_t = w_query.T.astype(BF16)
    per_step = tile // LANES
    packed = N_KEYS // 2
    blk = lambda i: (0, i, 0, 0)
    shape = lambda rows: jax.ShapeDtypeStruct((PEER_HEADS, t // LANES, rows, LANES), jnp.uint32)
    spec = lambda rows: pl.BlockSpec((PEER_HEADS, per_step, rows, LANES), blk)
    return pl.pallas_call(
        _peer_score_kernel,
        grid=(t // tile,),
        in_specs=[pl.BlockSpec((D_MODEL, tile), lambda i: (0, i)),
                  pl.BlockSpec((PEER_HEADS * D_KEY, D_MODEL), lambda i: (0, 0)),
                  pl.BlockSpec((N_KEYS, HALF_KEY), lambda i: (0, 0)),
                  pl.BlockSpec((N_KEYS, HALF_KEY), lambda i: (0, 0))],
        out_specs=[spec(N_KEYS), spec(N_KEYS), spec(packed), spec(packed)],
        out_shape=[shape(N_KEYS), shape(N_KEYS), shape(packed), shape(packed)],
        compiler_params=_cparams("parallel"),
        name="peer_scores",
    )(xt_bf, wq_t, keys1.astype(BF16), keys2.astype(BF16))


def _pair_rows(ref, h, c, i):
    sublanes, packed_rows = 8, 16
    tile = pltpu.bitcast(jnp.broadcast_to(ref[h, c, pl.ds(i, 1), :], (sublanes, LANES)), BF16)
    return jnp.tile(tile, (N_KEYS // packed_rows, 1))


def _peer_main_kernel(xb_ref, x_ref, u_ref, vt_prev_ref, vt_last_ref, e1_ref, n1_ref, e2_ref, r2_ref,
                      g_ref, b_ref, o_ref, acc_ref, w_ref, *, rows_per_step, rows_per_dot):
    ei = pl.program_id(1)
    cur = ei % 2
    zero = jnp.zeros((), BF16)

    @pl.when(ei == 0)
    def _():
        acc_ref[...] = jnp.zeros(acc_ref.shape, F32)
        w_ref[1] = jnp.zeros(w_ref.shape[1:], BF16)

    xb = xb_ref[...]
    acc_ref[...] += jnp.dot(vt_prev_ref[...], w_ref[1 - cur], preferred_element_type=F32)
    n_tok_tiles = xb.shape[1] // LANES
    for d in range(rows_per_step // rows_per_dot):
        d0 = d * rows_per_dot * N_KEYS
        hid = jnp.dot(u_ref[d0:d0 + rows_per_dot * N_KEYS, :], xb, preferred_element_type=F32)
        for il in range(rows_per_dot):
            i = ei * rows_per_step + d * rows_per_dot + il
            rows = slice(il * N_KEYS, (il + 1) * N_KEYS)
            for c in range(n_tok_tiles):
                cols = slice(c * LANES, (c + 1) * LANES)
                act = _gelu_tanh(hid[rows, cols].astype(BF16))
                gates = jnp.zeros(act.shape, BF16)
                for h in range(PEER_HEADS):
                    keep = pltpu.bitcast(r2_ref[h, c], BF16) < _pair_rows(n1_ref, h, c, i)
                    gates = gates + (jnp.where(keep, pltpu.bitcast(e2_ref[h, c], BF16), zero)
                                     * _pair_rows(e1_ref, h, c, i))
                w_ref[cur, d0 + il * N_KEYS:d0 + (il + 1) * N_KEYS, cols] = gates * act

    @pl.when(ei == pl.num_programs(1) - 1)
    def _():
        last = jnp.dot(vt_last_ref[...], w_ref[cur], preferred_element_type=F32)
        z = ALPHA * x_ref[...] + (acc_ref[...] + last)
        mu = jnp.mean(z, axis=0, keepdims=True)
        zc = z - mu
        var = jnp.mean(zc * zc, axis=0, keepdims=True)
        o_ref[...] = zc * lax.rsqrt(var + LN_EPS) * g_ref[...] + b_ref[...]


def _peer_main(xt_bf, xt, u_bf, vt_bf, scores, ln_g, ln_b, tile, e_tile):
    t = xt.shape[1]
    n_e = N_EXPERTS // e_tile
    per_step = tile // LANES
    tok = lambda ti, ei: (0, ti)
    score_spec = lambda rows: pl.BlockSpec((PEER_HEADS, per_step, rows, LANES), lambda ti, ei: (0, ti, 0, 0))
    return pl.pallas_call(
        functools.partial(_peer_main_kernel, rows_per_step=e_tile // N_KEYS, rows_per_dot=2),
        grid=(t // tile, n_e),
        in_specs=[pl.BlockSpec((D_MODEL, tile), tok),
                  pl.BlockSpec((D_MODEL, tile), tok),
                  pl.BlockSpec((e_tile, D_MODEL), lambda ti, ei: (ei, 0)),
                  pl.BlockSpec((D_MODEL, e_tile), lambda ti, ei: (0, jnp.maximum(ei - 1, 0))),
                  pl.BlockSpec((D_MODEL, e_tile), lambda ti, ei: (0, n_e - 1)),
                  score_spec(N_KEYS), score_spec(N_KEYS), score_spec(N_KEYS // 2), score_spec(N_KEYS // 2),
                  pl.BlockSpec((D_MODEL, 1), lambda ti, ei: (0, 0)),
                  pl.BlockSpec((D_MODEL, 1), lambda ti, ei: (0, 0))],
        out_specs=pl.BlockSpec((D_MODEL, tile), tok),
        out_shape=jax.ShapeDtypeStruct((D_MODEL, t), F32),
        scratch_shapes=[pltpu.VMEM((D_MODEL, tile), F32), pltpu.VMEM((2, e_tile, tile), BF16)],
        compiler_params=_cparams("parallel", "arbitrary"),
        name="peer_main",
    )(xt_bf, xt, u_bf, vt_bf, vt_bf, *scores, ln_g.reshape(-1, 1), ln_b.reshape(-1, 1))


def _layer(x2d, mem2d, cos_t, sin_t, batch, seq, p):
    tile = min(512, seq)
    q, k, v, u = _in_proj(x2d, cos_t, sin_t, p["w_in"], p["g_cq"], p["w_uq"], p["g_ckv"], p["w_ukv"], tile)
    o_att = _flash(q, k, v, batch, seq, min(4096, seq), min(512, seq))
    tables = _ssm_tables(p["lam_re"], p["lam_im"], p["log_dt"], p["b_re"], p["b_im"], p["c_re"],
                         p["c_im"], p["d_skip"])
    y_ssm = _ssm(u, batch, seq, tables)
    x1 = _mix_out(x2d, o_att, y_ssm, p["w_glu"], p["b_glu"], p["g_mla_out"], p["g_ssm_out"], p["w_out"],
                  p["ln1_g"], p["ln1_b"], tile)
    kx, vx = _mem_kv(mem2d, p["xk"], p["xv"])
    n_mem = mem2d.shape[0] // batch
    x2 = _xattn(x1, kx.reshape(batch, n_mem, D_MODEL), vx.reshape(batch, n_mem, D_MODEL), p["xq"], p["xo"],
                p["ln2_g"], p["ln2_b"], seq, tile)
    x2t = x2.T
    x2t_bf = x2t.astype(BF16)
    scores = _peer_scores(x2t_bf, p["w_query"], p["sub_keys1"], p["sub_keys2"], min(256, seq))
    out_t = _peer_main(x2t_bf, x2t, p["u_tab"].astype(BF16), p["v_tab"].T.astype(BF16), scores,
                       p["ln3_g"], p["ln3_b"], tile, 1024)
    return out_t.T


def kernel(x, mem, positions, w_in, g_cq, w_uq, g_ckv, w_ukv, lam_re, lam_im, log_dt, b_re, b_im, c_re, c_im, d_skip, w_glu, b_glu, g_mla_out, g_ssm_out, w_out, ln1_g, ln1_b, xq, xk, xv, xo, ln2_g, ln2_b, w_query, sub_keys1, sub_keys2, u_tab, v_tab, ln3_g, ln3_b):
    batch, seq, _ = x.shape
    params = dict(w_in=w_in, g_cq=g_cq, w_uq=w_uq, g_ckv=g_ckv, w_ukv=w_ukv, lam_re=lam_re, lam_im=lam_im,
                  log_dt=log_dt, b_re=b_re, b_im=b_im, c_re=c_re, c_im=c_im, d_skip=d_skip, w_glu=w_glu,
                  b_glu=b_glu, g_mla_out=g_mla_out, g_ssm_out=g_ssm_out, w_out=w_out, ln1_g=ln1_g,
                  ln1_b=ln1_b, xq=xq, xk=xk, xv=xv, xo=xo, ln2_g=ln2_g, ln2_b=ln2_b, w_query=w_query,
                  sub_keys1=sub_keys1, sub_keys2=sub_keys2, u_tab=u_tab, v_tab=v_tab, ln3_g=ln3_g,
                  ln3_b=ln3_b)
    cos_t, sin_t = _rope_tables(positions)
    h = x.reshape(batch * seq, D_MODEL)
    mem2d = mem.reshape(-1, D_MODEL)
    for l in range(w_in.shape[0]):
        h = _layer(h, mem2d, cos_t, sin_t, batch, seq, {k: v[l] for k, v in params.items()})
    return h.reshape(batch, seq, D_MODEL)
```
